```python
import jax, jax.numpy as jnp
from jax import lax
import numpy as np

D_MODEL = 2048
BATCH = 4
SEQ = 4096
DEPTH = 4

GRID_W = 64
CTX_LEN = 256
EPS = 1e-6
N_MOD = 6

MLA_HEADS = 8
MLA_NOPE = 128
MLA_ROPE = 64
MLA_V = 128
MLA_QK = MLA_NOPE + MLA_ROPE
Q_LORA = 512
KV_LORA = 256
MLA_WIDTH = MLA_HEADS * MLA_V
ATTN_BLOCK = 128
ROPE_THETA = 10000.0
ROPE_HALF = MLA_ROPE // 2
ROPE_AXIS = MLA_ROPE // 2

GLA_HEADS = 4
GLA_HEAD_K = 128
GLA_HEAD_V = 256
GLA_KEY = GLA_HEADS * GLA_HEAD_K
GLA_WIDTH = GLA_HEADS * GLA_HEAD_V
GK_RANK = 16
GK_NORMALIZER = 16.0
GLA_CHUNK = 64

MIX_WIDTH = MLA_WIDTH + GLA_WIDTH
IN_SPLITS = (Q_LORA, KV_LORA, MLA_ROPE, GLA_KEY, GLA_KEY, GLA_WIDTH, GLA_WIDTH, GK_RANK, GK_RANK)
IN_COLS = sum(IN_SPLITS)

PEER_HEADS = 8
PEER_DKEY = 256
N_KEYS = 128
N_EXPERTS = N_KEYS * N_KEYS
PEER_TOPK = 16
PEER_BLOCK = 128

kernel_name = 'hybrid_mla_gla_peer_dit'


def rmsnorm(x, g):
    xf = x.astype(jnp.float32)
    y = xf * lax.rsqrt(jnp.mean(xf * xf, axis=-1, keepdims=True) + EPS)
    return (y * g.astype(jnp.float32)).astype(x.dtype)


def modulate(x, g, shift, scale):
    return rmsnorm(x, g) * (1 + scale) + shift


def axial_rope_tables(t, dtype):
    rows = t // GRID_W
    row = jnp.repeat(jnp.arange(rows, dtype=jnp.float32), GRID_W)
    col = jnp.tile(jnp.arange(GRID_W, dtype=jnp.float32), rows)
    inv_freq = 1.0 / (ROPE_THETA ** (jnp.arange(0, ROPE_AXIS, 2, dtype=jnp.float32) / ROPE_AXIS))
    ang = jnp.concatenate([row[:, None] * inv_freq, col[:, None] * inv_freq], axis=-1)
    return jnp.cos(ang).astype(dtype), jnp.sin(ang).astype(dtype)


def apply_rope(x, cos, sin):
    x1, x2 = x[..., :ROPE_HALF], x[..., ROPE_HALF:]
    cos = cos[None, :, None, :]
    sin = sin[None, :, None, :]
    return jnp.concatenate([x1 * cos - x2 * sin, x1 * sin + x2 * cos], axis=-1)


def mixer_projections(h, w_in, q_norm_g, kv_norm_g, w_uq, w_ukv, w_gk2, b_gk, rope):
    b, t, _ = h.shape
    splits = [int(i) for i in np.cumsum(IN_SPLITS[:-1])]
    c_q, c_kv, k_r, gq, gk, gv, gg, lr_f, lr_b = jnp.split(h @ w_in, splits, axis=-1)
    q = (rmsnorm(c_q, q_norm_g) @ w_uq).reshape(b, t, MLA_HEADS, MLA_QK)
    kv = (rmsnorm(c_kv, kv_norm_g) @ w_ukv).reshape(b, t, MLA_HEADS, MLA_NOPE + MLA_V)
    q_nope, q_rope = q[..., :MLA_NOPE], q[..., MLA_NOPE:]
    k_nope, v = kv[..., :MLA_NOPE], kv[..., MLA_NOPE:]
    k_rope = k_r[:, :, None, :]
    if rope is not None:
        cos, sin = rope
        q_rope = apply_rope(q_rope, cos, sin)
        k_rope = apply_rope(k_rope, cos, sin)
    q = jnp.concatenate([q_nope, q_rope], axis=-1)
    k = jnp.concatenate([k_nope, jnp.broadcast_to(k_rope, (b, t, MLA_HEADS, MLA_ROPE))], axis=-1)

    def heads(z, d):
        return z.reshape(b, t, GLA_HEADS, d).transpose(0, 2, 1, 3)

    def log_decay(lr, i):
        z = (lr @ w_gk2[i] + b_gk[i]).astype(jnp.float32)
        return heads(jax.nn.log_sigmoid(z) / GK_NORMALIZER, GLA_HEAD_K)

    gla = (heads(gq, GLA_HEAD_K) * (GLA_HEAD_K ** -0.5), heads(gk, GLA_HEAD_K), heads(gv, GLA_HEAD_V),
           log_decay(lr_f, 0), log_decay(lr_b, 1), gg)
    return (q, k, v), gla


def attend(q, k, v):
    s = jnp.einsum('bqhd,bkhd->bhqk', q, k).astype(jnp.float32) * (MLA_QK ** -0.5)
    p = jax.nn.softmax(s, axis=-1).astype(v.dtype)
    return jnp.einsum('bhqk,bkhd->bqhd', p, v)


def latent_attention(q, k_all, v_all):
    b, t, h, d = q.shape
    nb = t // ATTN_BLOCK
    qb = q.reshape(b, nb, ATTN_BLOCK, h, d).transpose(1, 0, 2, 3, 4)
    ob = lax.map(lambda blk: attend(blk, k_all, v_all), qb)
    return ob.transpose(1, 0, 2, 3, 4).reshape(b, t, h * MLA_V)


def gla_chunked(q, k, v, loga, s0):
    b, h, t, dk = q.shape
    dv = v.shape[-1]
    n = t // GLA_CHUNK
    f32 = jnp.float32
    qc = q.reshape(b, h, n, GLA_CHUNK, dk).astype(f32)
    kc = k.reshape(b, h, n, GLA_CHUNK, dk).astype(f32)
    vc = v.reshape(b, h, n, GLA_CHUNK, dv).astype(f32)
    cum = jnp.cumsum(loga.reshape(b, h, n, GLA_CHUNK, dk).astype(f32), axis=3)
    last = cum[..., -1:, :]
    q_dec = qc * jnp.exp(cum)
    k_dec = kc * jnp.exp(-cum)
    k_end = kc * jnp.exp(last - cum)
    mask = jnp.tril(jnp.ones((GLA_CHUNK, GLA_CHUNK), dtype=bool))
    a = jnp.where(mask, jnp.einsum('bhncd,bhnjd->bhncj', q_dec, k_dec), 0.0)
    o_intra = jnp.einsum('bhncj,bhnje->bhnce', a, vc)
    ds = jnp.einsum('bhncd,bhnce->bhnde', k_end, vc)
    decay = jnp.exp(last[..., 0, :])

    def step(s, inp):
        dec_n, ds_n = inp
        return dec_n[..., None] * s + ds_n, s

    s_final, s_enter = lax.scan(step, s0.astype(f32), (jnp.moveaxis(decay, 2, 0), jnp.moveaxis(ds, 2, 0)))
    s_enter = jnp.moveaxis(s_enter, 0, 2)
    o = o_intra + jnp.einsum('bhncd,bhnde->bhnce', q_dec, s_enter)
    return o.reshape(b, h, t, dv), s_final


def gla_bidirectional(q, k, v, loga_f, loga_b, s0_f, s0_b):
    o_f, s_f = gla_chunked(q, k, v, loga_f, s0_f)
    flip = lambda z: jnp.flip(z, axis=2)
    o_b, s_b = gla_chunked(flip(q), flip(k), flip(v), flip(loga_b), s0_b)
    return o_f + flip(o_b), s_f, s_b


def gla_output(o, gg, norm_g):
    b, h, t, dv = o.shape
    o = rmsnorm(o.transpose(0, 2, 1, 3), norm_g).reshape(b, t, GLA_WIDTH)
    return (o * jax.nn.silu(gg)).astype(gg.dtype)


def token_mixer(h_lat, h_ctx, rope, w_in, q_norm_g, kv_norm_g, w_uq, w_ukv, w_gk2, b_gk,
                gla_norm_g, w_out, ctx_out):
    b, t, _ = h_lat.shape
    (q_l, k_l, v_l), gla_l = mixer_projections(h_lat, w_in, q_norm_g, kv_norm_g, w_uq, w_ukv, w_gk2, b_gk, rope)
    (q_c, k_c, v_c), gla_c = mixer_projections(h_ctx, w_in, q_norm_g, kv_norm_g, w_uq, w_ukv, w_gk2, b_gk, None)
    k_all = jnp.concatenate([k_c, k_l], axis=1)
    v_all = jnp.concatenate([v_c, v_l], axis=1)
    mla_l = latent_attention(q_l, k_all, v_all)
    zeros = jnp.zeros((b, GLA_HEADS, GLA_HEAD_K, GLA_HEAD_V), jnp.float32)
    o_c, s_f, s_b = gla_bidirectional(*gla_c[:5], zeros, zeros)
    o_l, _, _ = gla_bidirectional(*gla_l[:5], s_f, s_b)
    y_lat = jnp.concatenate([mla_l, gla_output(o_l, gla_l[5], gla_norm_g)], axis=-1) @ w_out
    if not ctx_out:
        return y_lat, None
    mla_c = attend(q_c, k_c, v_c).reshape(b, h_ctx.shape[1], MLA_WIDTH)
    y_ctx = jnp.concatenate([mla_c, gla_output(o_c, gla_c[5], gla_norm_g)], axis=-1) @ w_out
    return y_lat, y_ctx


def peer_block(h, w_query, sub_keys, expert_u, expert_v):
    m = h.shape[0]
    q = (h @ w_query).reshape(m, PEER_HEADS, 2, PEER_DKEY // 2)
    s1 = jnp.einsum('mhd,hkd->mhk', q[:, :, 0], sub_keys[0]).astype(jnp.float32)
    s2 = jnp.einsum('mhd,hkd->mhk', q[:, :, 1], sub_keys[1]).astype(jnp.float32)
    s1t, i1 = lax.top_k(s1, PEER_TOPK)
    s2t, i2 = lax.top_k(s2, PEER_TOPK)
    cand = (s1t[..., :, None] + s2t[..., None, :]).reshape(m, PEER_HEADS, PEER_TOPK * PEER_TOPK)
    cand_idx = (i1[..., :, None] * N_KEYS + i2[..., None, :]).reshape(m, PEER_HEADS, PEER_TOPK * PEER_TOPK)
    top_s, pos = lax.top_k(cand, PEER_TOPK)
    idx = jnp.take_along_axis(cand_idx, pos, axis=-1)
    gate = jax.nn.softmax(top_s, axis=-1).astype(h.dtype)
    act = jax.nn.gelu(jnp.einsum('mhkd,md->mhk', expert_u[idx], h)) * gate
    return jnp.einsum('mhk,mhkd->md', act, expert_v[idx])


def peer(h, w_query, sub_keys, expert_u, expert_v):
    b, t, d = h.shape
    hb = h.reshape(-1, PEER_BLOCK, d)
    out = lax.map(lambda blk: peer_block(blk, w_query, sub_keys, expert_u, expert_v), hb)
    return out.reshape(b, t, d)


def setup_inputs(seed: int = 0) -> dict:
    key = jax.random.key(seed)
    ks = jax.random.split(key, 22)
    f32 = jnp.float32
    nrm = lambda k, shape, s: jax.random.normal(k, shape, f32) * s
    gain = lambda k, shape: 1.0 + 0.02 * jax.random.normal(k, shape, f32)
    L, D = DEPTH, D_MODEL
    return {
        'x': nrm(ks[0], (BATCH, SEQ, D), 1.0),
        'c': nrm(ks[1], (BATCH, D), 1.0),
        'ctx': nrm(ks[2], (BATCH, CTX_LEN, D), 1.0),
        'c_ctx': nrm(ks[3], (D,), 1.0),
        'w_ada': nrm(ks[4], (L, D, N_MOD * D), 0.5 * D ** -0.5),
        'b_ada': nrm(ks[5], (L, N_MOD * D), 0.02),
        'norm_mix_g': gain(ks[6], (L, D)),
        'norm_ffn_g': gain(ks[7], (L, D)),
        'w_in': nrm(ks[8], (L, D, IN_COLS), D ** -0.5),
        'q_norm_g': gain(ks[9], (L, Q_LORA)),
        'kv_norm_g': gain(ks[10], (L, KV_LORA)),
        'w_uq': nrm(ks[11], (L, Q_LORA, MLA_HEADS * MLA_QK), Q_LORA ** -0.5),
        'w_ukv': nrm(ks[12], (L, KV_LORA, MLA_HEADS * (MLA_NOPE + MLA_V)), KV_LORA ** -0.5),
        'w_gk2': nrm(ks[13], (L, 2, GK_RANK, GLA_KEY), GK_RANK ** -0.5),
        'b_gk': nrm(ks[14], (L, 2, GLA_KEY), 0.1),
        'gla_norm_g': gain(ks[15], (L, GLA_HEAD_V)),
        'w_out': nrm(ks[16], (L, MIX_WIDTH, D), MIX_WIDTH ** -0.5),
        'w_query': nrm(ks[17], (L, D, PEER_HEADS * PEER_DKEY), D ** -0.5),
        'sub_keys': nrm(ks[18], (L, 2, PEER_HEADS, N_KEYS, PEER_DKEY // 2), (PEER_DKEY // 2) ** -0.5),
        'expert_u': nrm(ks[19], (L, N_EXPERTS, D), D ** -0.5),
        'expert_v': nrm(ks[20], (L, N_EXPERTS, D), D ** -0.5),
        'final_norm_g': gain(ks[21], (D,)),
    }


def reference(x, c, ctx, c_ctx, w_ada, b_ada, norm_mix_g, norm_ffn_g, w_in, q_norm_g, kv_norm_g,
              w_uq, w_ukv, w_gk2, b_gk, gla_norm_g, w_out, w_query, sub_keys, expert_u, expert_v,
              final_norm_g):
    t = x.shape[1]
    rope = axial_rope_tables(t, x.dtype)
    x_lat, x_ctx = x, ctx
    for l in range(DEPTH):
        ctx_out = l < DEPTH - 1
        sh_a, sc_a, g_a, sh_f, sc_f, g_f = jnp.split((jax.nn.silu(c) @ w_ada[l] + b_ada[l])[:, None, :], N_MOD, axis=-1)
        csh_a, csc_a, cg_a, csh_f, csc_f, cg_f = jnp.split(jax.nn.silu(c_ctx) @ w_ada[l] + b_ada[l], N_MOD, axis=-1)
        y_lat, y_ctx = token_mixer(modulate(x_lat, norm_mix_g[l], sh_a, sc_a),
                                   modulate(x_ctx, norm_mix_g[l], csh_a, csc_a), rope,
                                   w_in[l], q_norm_g[l], kv_norm_g[l], w_uq[l], w_ukv[l], w_gk2[l], b_gk[l],
                                   gla_norm_g[l], w_out[l], ctx_out)
        x_lat = x_lat + g_a * y_lat
        x_lat = x_lat + g_f * peer(modulate(x_lat, norm_ffn_g[l], sh_f, sc_f),
                                   w_query[l], sub_keys[l], expert_u[l], expert_v[l])
        if ctx_out:
            x_ctx = x_ctx + cg_a * y_ctx
            x_ctx = x_ctx + cg_f * peer(modulate(x_ctx, norm_ffn_g[l], csh_f, csc_f),
                                        w_query[l], sub_keys[l], expert_u[l], expert_v[l])
    return rmsnorm(x_lat, final_norm_g)
```

```python
import functools
import math

import jax
import jax.numpy as jnp
from jax import lax
from jax.experimental import pallas as pl
from jax.experimental.pallas import tpu as pltpu

F32 = jnp.float32
BF16 = jnp.bfloat16
I32 = jnp.int32

EPS = 1e-6
N_MOD = 6
GRID_W = 64
ROPE_THETA = 10000.0

MLA_HEADS = 8
MLA_NOPE = 128
MLA_ROPE = 64
MLA_V = 128
MLA_QK = MLA_NOPE + MLA_ROPE
MLA_HEAD_PAD = 256
Q_LORA = 512
KV_LORA = 256

GLA_HEADS = 4
GLA_HEAD_K = 128
GLA_HEAD_V = 256
GLA_KEY = GLA_HEADS * GLA_HEAD_K
GLA_WIDTH = GLA_HEADS * GLA_HEAD_V
GK_RANK = 16
GK_NORMALIZER = 16.0
GLA_CHUNK = 64

PEER_HEADS = 8
PEER_DKEY = 256
N_KEYS = 128
PEER_TOPK = 16

LANES = 128
VMEM_LIMIT = 56 * 1024 * 1024

P_CQ = 0
P_GQ = 512
P_GV = 1024
P_GG = 2048
P_GK = 3072
P_CKV = 3584
P_KR = 3840
P_LR = 3968
P_COLS = 4096


def _cparams(*sem):
    return pltpu.CompilerParams(dimension_semantics=sem, vmem_limit_bytes=VMEM_LIMIT)


def _pow2_tile(limit, *dims):
    t = 1
    while t * 2 <= limit and all(d % (t * 2) == 0 for d in dims):
        t *= 2
    return t


def _ada_kernel(c_ref, w_ref, b_ref, o_ref):
    c = c_ref[...]
    a = (c * jax.nn.sigmoid(c)).astype(BF16)
    o_ref[...] = jnp.dot(a, w_ref[...].astype(BF16), preferred_element_type=F32) + b_ref[...]


def ada_mods(cs, w_ada, b_ada):
    L, D, N = w_ada.shape
    R = cs.shape[0]
    tn = _pow2_tile(1024, N)
    return pl.pallas_call(
        _ada_kernel,
        grid=(L, N // tn),
        in_specs=[
            pl.BlockSpec((R, D), lambda l, j: (0, 0)),
            pl.BlockSpec((None, D, tn), lambda l, j: (l, 0, j)),
            pl.BlockSpec((None, 1, tn), lambda l, j: (l, 0, j)),
        ],
        out_specs=pl.BlockSpec((None, R, tn), lambda l, j: (l, 0, j)),
        out_shape=jax.ShapeDtypeStruct((L, R, N), F32),
        compiler_params=_cparams("parallel", "parallel"),
        name="ada_mods",
    )(cs, w_ada, b_ada.reshape(L, 1, N))


def _norm_matmul_kernel(*refs, has_mod, rope, emit_h, out_scale):
    it = iter(refs)
    x_ref = next(it)
    g_ref = next(it)
    sc_ref = next(it) if has_mod else None
    sh_ref = next(it) if has_mod else None
    w_ref = next(it)
    ct_ref = next(it) if rope else None
    st_ref = next(it) if rope else None
    o_ref = next(it)
    h_ref = next(it) if emit_h else None
    xn_ref = next(it)

    @pl.when(pl.program_id(1) == 0)
    def _():
        x = x_ref[...].astype(F32)
        ms = jnp.mean(x * x, axis=-1, keepdims=True)
        y = x * lax.rsqrt(ms + EPS) * g_ref[...]
        if has_mod:
            y = y * (1.0 + sc_ref[...]) + sh_ref[...]
        yb = y.astype(BF16)
        xn_ref[...] = yb
        if emit_h:
            h_ref[...] = yb

    y = jnp.dot(xn_ref[...], w_ref[...], preferred_element_type=F32)
    if rope:
        ct = ct_ref[...]
        st = st_ref[...]
        for k in range(y.shape[1] // MLA_HEAD_PAD):
            lo = k * MLA_HEAD_PAD
            yn = y[:, lo:lo + LANES]
            yr = y[:, lo + LANES:lo + 2 * LANES]
            rr = yr * ct + pltpu.roll(yr, LANES // 2, 1) * st
            o_ref[:, lo:lo + LANES] = (yn * out_scale).astype(o_ref.dtype)
            o_ref[:, lo + LANES:lo + 2 * LANES] = (rr * out_scale).astype(o_ref.dtype)
    else:
        o_ref[...] = y.astype(o_ref.dtype)


def norm_matmul(x, xcol, k, gain, w, *, tm, tn, mod=None, mod_row=None, rope=None, emit_h=False,
                out_dtype=F32, out_scale=1.0, name):
    m = x.shape[0]
    n = w.shape[1]
    has_mod = mod is not None
    in_specs = [pl.BlockSpec((tm, k), lambda i, j: (i, xcol)),
                pl.BlockSpec((1, k), lambda i, j: (0, 0))]
    args = [x, gain.reshape(1, k).astype(F32)]
    if has_mod:
        scale, shift = mod
        in_specs += [pl.BlockSpec((None, 1, k), lambda i, j: (mod_row(i), 0, 0))] * 2
        args += [scale, shift]
    in_specs.append(pl.BlockSpec((k, tn), lambda i, j: (0, j)))
    args.append(w)
    if rope is not None:
        in_specs += [pl.BlockSpec((tm, LANES), lambda i, j: (i, 0))] * 2
        args += list(rope)
    out_specs = [pl.BlockSpec((tm, tn), lambda i, j: (i, j))]
    out_shape = [jax.ShapeDtypeStruct((m, n), out_dtype)]
    if emit_h:
        out_specs.append(pl.BlockSpec((tm, k), lambda i, j: (i, 0)))
        out_shape.append(jax.ShapeDtypeStruct((m, k), BF16))
    res = pl.pallas_call(
        functools.partial(_norm_matmul_kernel, has_mod=has_mod, rope=rope is not None, emit_h=emit_h,
                          out_scale=out_scale),
        grid=(m // tm, n // tn),
        in_specs=in_specs,
        out_specs=out_specs,
        out_shape=out_shape,
        scratch_shapes=[pltpu.VMEM((tm, k), BF16)],
        compiler_params=_cparams("parallel", "arbitrary"),
        name=name,
    )(*args)
    return res if emit_h else res[0]


def _rope_k_kernel(p_ref, ct_ref, st_ref, o_ref):
    y = p_ref[...]
    o_ref[...] = (y * ct_ref[...] + pltpu.roll(y, LANES // 2, 1) * st_ref[...]).astype(o_ref.dtype)


def rope_k(p, ct, st, *, tm):
    m = p.shape[0]
    return pl.pallas_call(
        _rope_k_kernel,
        grid=(m // tm,),
        in_specs=[pl.BlockSpec((tm, LANES), lambda i: (i, P_KR // LANES)),
                  pl.BlockSpec((tm, LANES), lambda i: (i, 0)),
                  pl.BlockSpec((tm, LANES), lambda i: (i, 0))],
        out_specs=pl.BlockSpec((tm, LANES), lambda i: (i, 0)),
        out_shape=jax.ShapeDtypeStruct((m, LANES), BF16),
        compiler_params=_cparams("parallel"),
        name="rope_k",
    )(p, ct, st)


def _attn_kernel(*refs, n_seg):
    q_ref = refs[0]
    seg_refs = [refs[1 + 3 * s:4 + 3 * s] for s in range(n_seg)]
    o_ref = refs[1 + 3 * n_seg]
    kf_refs = refs[2 + 3 * n_seg:]

    @pl.when(pl.program_id(2) == 0)
    def _():
        for (kn_ref, _, kr_ref), kf_ref in zip(seg_refs, kf_refs):
            kf_ref[:, :LANES] = kn_ref[...]
            kf_ref[:, LANES:] = kr_ref[...]

    q = q_ref[...]
    scores = [lax.dot_general(q, kf_ref[...], (((1,), (1,)), ((), ())), preferred_element_type=F32)
              for kf_ref in kf_refs]
    mx = functools.reduce(jnp.maximum, [jnp.max(s, axis=-1, keepdims=True) for s in scores])
    probs = [jnp.exp(s - mx) for s in scores]
    den = functools.reduce(jnp.add, [jnp.sum(p, axis=-1, keepdims=True) for p in probs])
    inv = 1.0 / den
    acc = None
    for p, (_, v_ref, _) in zip(probs, seg_refs):
        pv = jnp.dot((p * inv).astype(BF16), v_ref[...], preferred_element_type=F32)
        acc = pv if acc is None else acc + pv
    o_ref[...] = acc.astype(o_ref.dtype)


def mla_attention(q, kv, kr, *, q_rows, seg_rows, batch, tq):
    q0, qn = q_rows
    nq = qn // tq
    in_specs = [pl.BlockSpec((tq, MLA_HEAD_PAD), lambda b, h, i: (q0 // tq + b * nq + i, h))]
    args = [q]
    scratch = []
    for s0, sn in seg_rows:
        in_specs += [
            pl.BlockSpec((sn, LANES), lambda b, h, i, s0=s0, sn=sn: (s0 // sn + b, 2 * h)),
            pl.BlockSpec((sn, LANES), lambda b, h, i, s0=s0, sn=sn: (s0 // sn + b, 2 * h + 1)),
            pl.BlockSpec((sn, LANES), lambda b, h, i, s0=s0, sn=sn: (s0 // sn + b, 0)),
        ]
        args += [kv, kv, kr]
        scratch.append(pltpu.VMEM((sn, MLA_HEAD_PAD), BF16))
    return pl.pallas_call(
        functools.partial(_attn_kernel, n_seg=len(seg_rows)),
        grid=(batch, MLA_HEADS, nq),
        in_specs=in_specs,
        out_specs=pl.BlockSpec((tq, MLA_V), lambda b, h, i: (b * nq + i, h)),
        out_shape=jax.ShapeDtypeStruct((batch * qn, MLA_HEADS * MLA_V), BF16),
        scratch_shapes=scratch,
        compiler_params=_cparams("parallel", "parallel", "arbitrary"),
        name=f"mla_attention_{len(seg_rows)}seg",
    )(*args)


def _gla_kernel(q_ref, k_ref, v_ref, lr_ref, wgk_ref, bgk_ref, o_ref, s_ref, *, blk):
    fwd = pl.program_id(1) == 0

    @pl.when(pl.program_id(2) == 0)
    def _():
        s_ref[...] = jnp.zeros_like(s_ref)

    c_len = GLA_CHUNK
    n_chunks = blk // c_len
    row = lax.broadcasted_iota(I32, (c_len, c_len), 0)
    col = lax.broadcasted_iota(I32, (c_len, c_len), 1)
    mask = jnp.where(fwd, row - col, col - row) >= 0
    tri = mask.astype(F32)
    hi = lax.Precision.HIGHEST
    for c in range(n_chunks):
        ci = jnp.where(fwd, c, n_chunks - 1 - c)
        rows = pl.ds(pl.multiple_of(ci * c_len, c_len), c_len)
        z = jnp.dot(lr_ref[rows, :].astype(BF16), wgk_ref[...], preferred_element_type=F32) + bgk_ref[...]
        loga = -(jnp.maximum(-z, 0.0) + jnp.log(1.0 + jnp.exp(-jnp.abs(z)))) / GK_NORMALIZER
        for h in range(GLA_HEADS):
            ks = slice(h * GLA_HEAD_K, (h + 1) * GLA_HEAD_K)
            vs = slice(h * GLA_HEAD_V, (h + 1) * GLA_HEAD_V)
            la = loga[:, ks]
            q = q_ref[rows, ks] * (GLA_HEAD_K ** -0.5)
            k = k_ref[rows, ks]
            v = v_ref[rows, vs].astype(BF16)
            cum = jnp.dot(tri, la, precision=hi, preferred_element_type=F32)
            zt = jnp.concatenate([k, la], axis=0).T
            k_t = zt[:, :c_len]
            la_t = zt[:, c_len:]
            cum_t = lax.dot_general(la_t, tri, (((1,), (1,)), ((), ())), precision=hi,
                                    preferred_element_type=F32)
            total_t = jnp.sum(la_t, axis=1, keepdims=True)
            q_dec = (q * jnp.exp(cum)).astype(BF16)
            k_dec = (k * jnp.exp(-cum)).astype(BF16)
            a = lax.dot_general(q_dec, k_dec, (((1,), (1,)), ((), ())), preferred_element_type=F32)
            a = jnp.where(mask, a, 0.0).astype(BF16)
            s_old = s_ref[h]
            o = (jnp.dot(a, v, preferred_element_type=F32)
                 + jnp.dot(q_dec, s_old.astype(BF16), preferred_element_type=F32))
            k_end_t = (k_t * jnp.exp(total_t - cum_t)).astype(BF16)
            s_ref[h] = jnp.exp(total_t) * s_old + jnp.dot(k_end_t, v, preferred_element_type=F32)
            o_ref[rows, vs] = o


def gla_scan(p, wgk, bgk, *, batch, t_lat, t_ctx):
    m = p.shape[0]
    blk = _pow2_tile(256, t_lat, t_ctx)
    nl, nc = t_lat // blk, t_ctx // blk

    def row_block(b, d, j):
        in_ctx = j < nc
        jc = jnp.where(d == 0, j, nc - 1 - j)
        jl = jnp.where(d == 0, j - nc, nl - 1 - (j - nc))
        return jnp.where(in_ctx, batch * nl + b * nc + jc, b * nl + jl)

    def col_spec(width, off):
        return pl.BlockSpec((blk, width), lambda b, d, j: (row_block(b, d, j), off // width))

    return pl.pallas_call(
        functools.partial(_gla_kernel, blk=blk),
        grid=(batch, 2, nc + nl),
        in_specs=[col_spec(GLA_KEY, P_GQ), col_spec(GLA_KEY, P_GK), col_spec(GLA_WIDTH, P_GV),
                  col_spec(LANES, P_LR),
                  pl.BlockSpec((None, LANES, GLA_KEY), lambda b, d, j: (d, 0, 0)),
                  pl.BlockSpec((None, 1, GLA_KEY), lambda b, d, j: (d, 0, 0))],
        out_specs=pl.BlockSpec((None, blk, GLA_WIDTH), lambda b, d, j: (d, row_block(b, d, j), 0)),
        out_shape=jax.ShapeDtypeStruct((2, m, GLA_WIDTH), F32),
        scratch_shapes=[pltpu.VMEM((GLA_HEADS, GLA_HEAD_K, GLA_HEAD_V), F32)],
        compiler_params=_cparams("parallel", "parallel", "arbitrary"),
        name="gla_scan",
    )(p, p, p, p, wgk, bgk)


def _mix_out_kernel(x_ref, mla_ref, o2_ref, gg_ref, gn_ref, w_ref, gate_ref, o_ref, a_ref):
    @pl.when(pl.program_id(1) == 0)
    def _():
        mla_w = mla_ref.shape[1]
        a_ref[:, :mla_w] = mla_ref[...]
        gn = gn_ref[...]
        for h in range(GLA_HEADS):
            vs = slice(h * GLA_HEAD_V, (h + 1) * GLA_HEAD_V)
            o = o2_ref[0, :, vs] + o2_ref[1, :, vs]
            ms = jnp.mean(o * o, axis=-1, keepdims=True)
            y = o * lax.rsqrt(ms + EPS) * gn
            gg = gg_ref[:, vs]
            a_ref[:, mla_w + h * GLA_HEAD_V:mla_w + (h + 1) * GLA_HEAD_V] = (
                y * (gg * jax.nn.sigmoid(gg))).astype(BF16)

    y = jnp.dot(a_ref[...], w_ref[...], preferred_element_type=F32)
    o_ref[...] = x_ref[...] + gate_ref[...] * y


def mix_out(x, mla, o2, p, gla_norm_g, w_out, gate, mod_row, *, tm, tn):
    m, d = x.shape
    kdim = w_out.shape[0]
    return pl.pallas_call(
        _mix_out_kernel,
        grid=(m // tm, d // tn),
        in_specs=[pl.BlockSpec((tm, tn), lambda i, j: (i, j)),
                  pl.BlockSpec((tm, mla.shape[1]), lambda i, j: (i, 0)),
                  pl.BlockSpec((2, tm, GLA_WIDTH), lambda i, j: (0, i, 0)),
                  pl.BlockSpec((tm, GLA_WIDTH), lambda i, j: (i, P_GG // GLA_WIDTH)),
                  pl.BlockSpec((1, GLA_HEAD_V), lambda i, j: (0, 0)),
                  pl.BlockSpec((kdim, tn), lambda i, j: (0, j)),
                  pl.BlockSpec((None, 1, tn), lambda i, j: (mod_row(i), 0, j))],
        out_specs=pl.BlockSpec((tm, tn), lambda i, j: (i, j)),
        out_shape=jax.ShapeDtypeStruct((m, d), F32),
        scratch_shapes=[pltpu.VMEM((tm, kdim), BF16)],
        compiler_params=_cparams("parallel", "arbitrary"),
        name="mix_out",
    )(x, mla, o2, p, gla_norm_g.reshape(1, GLA_HEAD_V), w_out, gate)


def _extract_topk(s, rank_id, payload, n, val_ref, pay_ref):
    big = jnp.int32(2 ** 30)

    def body(a, s):
        m = jnp.max(s, axis=0, keepdims=True)
        sel = jnp.min(jnp.where(s == m, rank_id, big), axis=0, keepdims=True)
        hit = rank_id == sel
        val_ref[pl.ds(a, 1), :] = m
        if payload is None:
            pay_ref[pl.ds(a, 1), :] = sel
        else:
            pay_ref[pl.ds(a, 1), :] = jnp.max(jnp.where(hit, payload, -1), axis=0, keepdims=True)
        return jnp.where(hit, -jnp.inf, s)

    lax.fori_loop(0, n, body, s)


def _route_kernel(q_ref, keys_ref, i1_ref, i2_ref, gate_ref, v1_ref, v2_ref, k1_ref, k2_ref, ts_ref, te_ref):
    half = PEER_DKEY // 2
    tk = PEER_TOPK
    tr = q_ref.shape[0]
    key_id = lax.broadcasted_iota(I32, (N_KEYS, tr), 0)
    for side, (v_ref, k_ref) in enumerate(((v1_ref, k1_ref), (v2_ref, k2_ref))):
        qs = q_ref[:, side * half:(side + 1) * half].astype(BF16)
        s = lax.dot_general(keys_ref[side], qs, (((1,), (1,)), ((), ())), preferred_element_type=F32)
        _extract_topk(s, key_id, None, tk, v_ref, k_ref)

    s1, s2 = v1_ref[...], v2_ref[...]
    e1, e2 = k1_ref[...] * N_KEYS, k2_ref[...]
    hk = tk // 2
    cand = [s1[0:1] + s2] + [s1[a:a + 1] + s2[0:hk] for a in range(1, hk)] + [s1[hk:] + s2[0:1]]
    eid = [e1[0:1] + e2] + [e1[a:a + 1] + e2[0:hk] for a in range(1, hk)] + [e1[hk:] + e2[0:1]]
    cand = jnp.concatenate(cand, axis=0)
    eid = jnp.concatenate(eid, axis=0)
    r = lax.broadcasted_iota(I32, cand.shape, 0)
    mid = r - tk
    flat = jnp.where(r < tk, r,
                     jnp.where(r < tk + (hk - 1) * hk, (1 + (mid >> 3)) * tk + (mid & (hk - 1)),
                               (hk + r - (tk + (hk - 1) * hk)) * tk))
    _extract_topk(cand, flat, eid, tk, ts_ref, te_ref)

    ts = ts_ref[...]
    ex = jnp.exp(ts - ts[0:1])
    gate_ref[...] = ex / jnp.sum(ex, axis=0, keepdims=True)
    te = te_ref[...]
    i1_ref[...] = te >> 7
    i2_ref[...] = te & (N_KEYS - 1)


def peer_route(q, keys, *, tr):
    m = q.shape[0]
    tk = PEER_TOPK
    out_spec = pl.BlockSpec((tk, tr), lambda i, h: (h, i))
    return pl.pallas_call(
        _route_kernel,
        grid=(m // tr, PEER_HEADS),
        in_specs=[pl.BlockSpec((tr, PEER_DKEY), lambda i, h: (i, h)),
                  pl.BlockSpec((2, None, N_KEYS, PEER_DKEY // 2), lambda i, h: (0, h, 0, 0))],
        out_specs=[out_spec, out_spec, out_spec],
        out_shape=[jax.ShapeDtypeStruct((PEER_HEADS * tk, m), I32),
                   jax.ShapeDtypeStruct((PEER_HEADS * tk, m), I32),
                   jax.ShapeDtypeStruct((PEER_HEADS * tk, m), F32)],
        scratch_shapes=[pltpu.VMEM((tk, tr), F32), pltpu.VMEM((tk, tr), F32),
                        pltpu.VMEM((tk, tr), I32), pltpu.VMEM((tk, tr), I32),
                        pltpu.VMEM((tk, tr), F32), pltpu.VMEM((tk, tr), I32)],
        compiler_params=_cparams("parallel", "arbitrary"),
        name="peer_route",
    )(q, keys)


GATE_GROUP = 16


def _gate_matrix_kernel(i1_ref, i2_ref, g_ref, o_ref, i1s_ref, i2s_ref, gs_ref, scr_ref):
    tg = o_ref.shape[0]
    i1s_ref[...] = i1_ref[...].astype(F32).T
    i2s_ref[...] = i2_ref[...].astype(F32).T
    gs_ref[...] = g_ref[...].T
    slot = lax.broadcasted_iota(I32, (N_KEYS, N_KEYS), 0).astype(F32)

    def group(g, carry):
        base = pl.multiple_of(g * GATE_GROUP, GATE_GROUP)
        for t in range(GATE_GROUP):
            row = pl.ds(base + t, 1)
            lt = (slot == i1s_ref[row, :]).astype(BF16)
            rt = jnp.where(slot == i2s_ref[row, :], gs_ref[row, :], 0.0).astype(BF16)
            r = lax.dot_general(lt, rt, (((1,), (1,)), ((), ())), preferred_element_type=F32)
            scr_ref[pl.ds(t, N_KEYS, stride=GATE_GROUP), :] = r
        for a in range(N_KEYS):
            o_ref[pl.ds(base, GATE_GROUP), a * N_KEYS:(a + 1) * N_KEYS] = (
                scr_ref[a * GATE_GROUP:(a + 1) * GATE_GROUP, :].astype(o_ref.dtype))
        return carry

    lax.fori_loop(0, tg // GATE_GROUP, group, 0)


def gate_matrix(i1, i2, gate, *, tg):
    hk, m = i1.shape
    spec = pl.BlockSpec((hk, tg), lambda i: (0, i))
    return pl.pallas_call(
        _gate_matrix_kernel,
        grid=(m // tg,),
        in_specs=[spec, spec, spec],
        out_specs=pl.BlockSpec((tg, N_KEYS * N_KEYS), lambda i: (i, 0)),
        out_shape=jax.ShapeDtypeStruct((m, N_KEYS * N_KEYS), BF16),
        scratch_shapes=[pltpu.VMEM((tg, hk), F32), pltpu.VMEM((tg, hk), F32), pltpu.VMEM((tg, hk), F32),
                        pltpu.VMEM((N_KEYS * GATE_GROUP, N_KEYS), F32)],
        compiler_params=_cparams("parallel"),
        name="gate_matrix",
    )(i1, i2, gate)


def _gelu_tanh(x):
    c = math.sqrt(2.0 / math.pi)
    return 0.5 * x * (1.0 + jnp.tanh(c * (x + 0.044715 * (x * x * x))))


def _peer_dense_kernel(h_ref, u_ref, v_ref, g_ref, x_ref, gate_ref, o_ref, acc_ref):
    c = pl.program_id(1)

    @pl.when(c == 0)
    def _():
        acc_ref[...] = jnp.zeros_like(acc_ref)

    s = lax.dot_general(h_ref[...], u_ref[...], (((1,), (1,)), ((), ())), preferred_element_type=F32)
    w = (_gelu_tanh(s) * g_ref[...].astype(F32)).astype(BF16)
    acc_ref[...] += jnp.dot(w, v_ref[...], preferred_element_type=F32)

    @pl.when(c == pl.num_programs(1) - 1)
    def _():
        o_ref[...] = x_ref[...] + gate_ref[...] * acc_ref[...]


def peer_dense(h, u, v, g, x, gate, mod_row, *, tm, tn):
    m, d = x.shape
    n_exp = u.shape[0]
    return pl.pallas_call(
        _peer_dense_kernel,
        grid=(m // tm, n_exp // tn),
        in_specs=[pl.BlockSpec((tm, d), lambda i, c: (i, 0)),
                  pl.BlockSpec((tn, d), lambda i, c: (c, 0)),
                  pl.BlockSpec((tn, d), lambda i, c: (c, 0)),
                  pl.BlockSpec((tm, tn), lambda i, c: (i, c)),
                  pl.BlockSpec((tm, d), lambda i, c: (i, 0)),
                  pl.BlockSpec((None, 1, d), lambda i, c: (mod_row(i), 0, 0))],
        out_specs=pl.BlockSpec((tm, d), lambda i, c: (i, 0)),
        out_shape=jax.ShapeDtypeStruct((m, d), F32),
        scratch_shapes=[pltpu.VMEM((tm, d), F32)],
        compiler_params=_cparams("parallel", "arbitrary"),
        name="peer_dense",
    )(h, u, v, g, x, gate)


def _rmsnorm_kernel(x_ref, g_ref, o_ref):
    x = x_ref[...]
    ms = jnp.mean(x * x, axis=-1, keepdims=True)
    o_ref[...] = x * lax.rsqrt(ms + EPS) * g_ref[...]


def final_rmsnorm(x, g, *, rows, tm):
    d = x.shape[1]
    return pl.pallas_call(
        _rmsnorm_kernel,
        grid=(rows // tm,),
        in_specs=[pl.BlockSpec((tm, d), lambda i: (i, 0)), pl.BlockSpec((1, d), lambda i: (0, 0))],
        out_specs=pl.BlockSpec((tm, d), lambda i: (i, 0)),
        out_shape=jax.ShapeDtypeStruct((rows, d), F32),
        compiler_params=_cparams("parallel"),
        name="final_rmsnorm",
    )(x, g.reshape(1, d))


def _rot_cols(w):
    half = MLA_ROPE // 2
    return jnp.concatenate([-w[..., half:], w[..., :half]], axis=-1)


def _layout_w_in(w_in):
    d = w_in.shape[0]
    o = [0]
    for wdt in (Q_LORA, KV_LORA, MLA_ROPE, GLA_KEY, GLA_KEY, GLA_WIDTH, GLA_WIDTH, GK_RANK, GK_RANK):
        o.append(o[-1] + wdt)
    c_q, c_kv, k_r, gq, gk, gv, gg, lr_f, lr_b = (w_in[:, o[i]:o[i + 1]] for i in range(9))
    pad = jnp.zeros((d, LANES - 2 * GK_RANK), w_in.dtype)
    return jnp.concatenate([c_q, gq, gv, gg, gk, c_kv, k_r, _rot_cols(k_r), lr_f, lr_b, pad], axis=1).astype(BF16)


def _layout_w_uq(w_uq):
    k = w_uq.shape[0]
    w = w_uq.reshape(k, MLA_HEADS, MLA_QK)
    nope, rope = w[..., :MLA_NOPE], w[..., MLA_NOPE:]
    return jnp.concatenate([nope, rope, _rot_cols(rope)], axis=-1).reshape(k, MLA_HEADS * MLA_HEAD_PAD).astype(BF16)


def _layout_w_gk(w_gk2):
    out = jnp.zeros((2, LANES, GLA_KEY), w_gk2.dtype)
    out = out.at[0, :GK_RANK].set(w_gk2[0])
    out = out.at[1, GK_RANK:2 * GK_RANK].set(w_gk2[1])
    return out.astype(BF16)


def _rope_tables(t_lat, n_ctx_rows, batch):
    rows = t_lat // GRID_W
    row = jnp.repeat(jnp.arange(rows, dtype=F32), GRID_W)
    col = jnp.tile(jnp.arange(GRID_W, dtype=F32), rows)
    axis = MLA_ROPE // 2
    inv_freq = 1.0 / (ROPE_THETA ** (jnp.arange(0, axis, 2, dtype=F32) / axis))
    ang = jnp.concatenate([row[:, None] * inv_freq, col[:, None] * inv_freq], axis=-1)
    zeros = jnp.zeros((t_lat, LANES - MLA_ROPE), F32)
    ct = jnp.concatenate([jnp.cos(ang), jnp.cos(ang), zeros], axis=-1)
    st = jnp.concatenate([jnp.sin(ang), jnp.sin(ang), zeros], axis=-1)
    ct_ctx = jnp.concatenate([jnp.ones((n_ctx_rows, MLA_ROPE), F32), jnp.zeros((n_ctx_rows, LANES - MLA_ROPE), F32)], -1)
    st_ctx = jnp.zeros((n_ctx_rows, LANES), F32)
    return (jnp.concatenate([jnp.tile(ct, (batch, 1)), ct_ctx], axis=0),
            jnp.concatenate([jnp.tile(st, (batch, 1)), st_ctx], axis=0))


def kernel(x, c, ctx, c_ctx, w_ada, b_ada, norm_mix_g, norm_ffn_g, w_in, q_norm_g, kv_norm_g, w_uq, w_ukv,
           w_gk2, b_gk, gla_norm_g, w_out, w_query, sub_keys, expert_u, expert_v, final_norm_g):
    batch, t_lat, d = x.shape
    t_ctx = ctx.shape[1]
    depth = w_ada.shape[0]
    n_lat, n_ctx = batch * t_lat, batch * t_ctx
    m = n_lat + n_ctx

    tm = _pow2_tile(512, t_lat, n_ctx)

    def mod_row(i):
        return jnp.where(i < n_lat // tm, i // (t_lat // tm), batch)

    n_mod_rows = 8 * ((batch + 1 + 7) // 8)
    cs = jnp.zeros((n_mod_rows, d), F32).at[:batch].set(c).at[batch].set(c_ctx)
    mods = ada_mods(cs, w_ada, b_ada).reshape(depth, n_mod_rows, N_MOD, 1, d)
    ct, st = _rope_tables(t_lat, n_ctx, batch)

    xa = jnp.concatenate([x.reshape(n_lat, d), ctx.reshape(n_ctx, d)], axis=0)
    for l in range(depth):
        sh_a, sc_a, g_a, sh_f, sc_f, g_f = (mods[l, :, i] for i in range(N_MOD))
        p = norm_matmul(xa, 0, d, norm_mix_g[l], _layout_w_in(w_in[l]), tm=tm, tn=512,
                        mod=(sc_a, sh_a), mod_row=mod_row, name="in_proj")
        q = norm_matmul(p, P_CQ // Q_LORA, Q_LORA, q_norm_g[l], _layout_w_uq(w_uq[l]), tm=tm, tn=512,
                        rope=(ct, st), out_dtype=BF16, out_scale=MLA_QK ** -0.5, name="q_proj")
        kv = norm_matmul(p, P_CKV // KV_LORA, KV_LORA, kv_norm_g[l], w_ukv[l].astype(BF16), tm=tm, tn=512,
                         out_dtype=BF16, name="kv_proj")
        kr = rope_k(p, ct, st, tm=tm)
        lat_rows, ctx_rows = (0, t_lat), (n_lat, t_ctx)
        mla_l = mla_attention(q, kv, kr, q_rows=lat_rows, seg_rows=(ctx_rows, lat_rows), batch=batch,
                              tq=_pow2_tile(512, t_lat))
        mla_c = mla_attention(q, kv, kr, q_rows=ctx_rows, seg_rows=(ctx_rows,), batch=batch,
                              tq=_pow2_tile(512, t_ctx))
        mla = jnp.concatenate([mla_l, mla_c], axis=0)
        o2 = gla_scan(p, _layout_w_gk(w_gk2[l]), b_gk[l].reshape(2, 1, GLA_KEY), batch=batch, t_lat=t_lat,
                      t_ctx=t_ctx)
        xa = mix_out(xa, mla, o2, p, gla_norm_g[l], w_out[l].astype(BF16), g_a, mod_row, tm=tm, tn=512)

        qp, h = norm_matmul(xa, 0, d, norm_ffn_g[l], w_query[l].astype(BF16), tm=tm, tn=512,
                            mod=(sc_f, sh_f), mod_row=mod_row, emit_h=True, name="peer_query")
        keys = sub_keys[l].astype(BF16)
        i1, i2, gate = peer_route(qp, keys, tr=_pow2_tile(256, m))
        gmat = gate_matrix(i1, i2, gate, tg=_pow2_tile(128, m))
        xa = peer_dense(h, expert_u[l].astype(BF16), expert_v[l].astype(BF16), gmat, xa, g_f, mod_row,
                        tm=tm, tn=512)
    return final_rmsnorm(xa, final_norm_g, rows=n_lat, tm=tm).reshape(batch, t_lat, d)
```
